```python
import math
import jax, jax.numpy as jnp
from jax import lax
import numpy as np

D_MODEL = 1024
BATCH = 8
SEQ = 2048
DEPTH = 1

CHUNK = 64
Q_BLOCK = 128
ATTN_WIDTH = D_MODEL // 2
POOL_WIDTH = D_MODEL - ATTN_WIDTH
DIFF_HEAD_DIM = 64
N_ATTN_HEADS = ATTN_WIDTH // (2 * DIFF_HEAD_DIM)
POOL_WINDOWS = (2, 4, 8, 16)
N_POOL_GROUPS = len(POOL_WINDOWS)
POOL_GROUP = POOL_WIDTH // N_POOL_GROUPS
IN_COLS = 3 * ATTN_WIDTH + POOL_WIDTH
N_BUCKETS = 32
MAX_DISTANCE = 128
N_EXPERTS = 32
TOP_K = 4
D_FF = D_MODEL
SWIGLU_LIMIT = 7.0
SWIGLU_ALPHA = 1.702
EXPERT_BLOCK = 128
DEEPNORM_ALPHA = (2.0 * DEPTH) ** 0.25
DEEPNORM_BETA = (8.0 * DEPTH) ** -0.25
LN_EPS = 1e-5

kernel_name = "hybrid_diffattn_pool_moe_deepnorm_adaln"


def layer_norm(x, g=None, b=None, eps=LN_EPS):
    xf = x.astype(jnp.float32)
    mu = jnp.mean(xf, axis=-1, keepdims=True)
    var = jnp.mean(jnp.square(xf - mu), axis=-1, keepdims=True)
    y = (xf - mu) * lax.rsqrt(var + eps)
    if g is not None:
        y = y * g.astype(jnp.float32) + b.astype(jnp.float32)
    return y.astype(x.dtype)


def rms_norm(x, g, eps=LN_EPS):
    xf = x.astype(jnp.float32)
    y = xf * lax.rsqrt(jnp.mean(jnp.square(xf), axis=-1, keepdims=True) + eps)
    return (y * g.astype(jnp.float32)).astype(x.dtype)


def modulate(h, shift, scale):
    return h * (1.0 + scale[:, None, :]) + shift[:, None, :]


def t5_bucket(rel):
    nb = N_BUCKETS // 2
    ret = (rel > 0).astype(jnp.int32) * nb
    n = jnp.abs(rel)
    max_exact = nb // 2
    nf = jnp.maximum(n, 1).astype(jnp.float32)
    large = max_exact + (jnp.log(nf / max_exact) / math.log(MAX_DISTANCE / max_exact)
                         * (nb - max_exact)).astype(jnp.int32)
    large = jnp.minimum(large, nb - 1)
    return ret + jnp.where(n < max_exact, n, large)


def diff_attention(q, k, v, lam, rel_bias):
    S = q.shape[1]
    scale = DIFF_HEAD_DIM ** -0.5
    neg = jnp.finfo(jnp.float32).min
    outs = []
    for i in range(S // Q_BLOCK):
        q0 = i * Q_BLOCK
        kend = q0 + Q_BLOCK
        qb = q[:, q0:kend]
        kb = k[:, :kend]
        vb = v[:, :kend]
        s = jnp.einsum('bqhmd,bkhmd->bhmqk', qb, kb).astype(jnp.float32) * scale
        qpos = q0 + jnp.arange(Q_BLOCK, dtype=jnp.int32)
        kpos = jnp.arange(kend, dtype=jnp.int32)
        bias = rel_bias.astype(jnp.float32)[t5_bucket(kpos[None, :] - qpos[:, None])]
        bias = jnp.transpose(bias, (2, 0, 1))
        mask = (kpos[None, :] // CHUNK) <= (qpos[:, None] // CHUNK)
        s = jnp.where(mask, s + bias[None, :, None], neg)
        p = jax.nn.softmax(s, axis=-1)
        a = p[:, :, 0] - lam * p[:, :, 1]
        outs.append(jnp.einsum('bhqk,bkhe->bqhe', a.astype(vb.dtype), vb))
    return jnp.concatenate(outs, axis=1)


def pool_mixer(u, w_pool, b_pool, pool_scale):
    B, S, C = u.shape
    uf = u.astype(jnp.float32)
    cs = jnp.concatenate([jnp.zeros((B, 1, C), jnp.float32), jnp.cumsum(uf, axis=1)], axis=1)
    t = jnp.arange(S, dtype=jnp.int32)
    groups = []
    for g, w in enumerate(POOL_WINDOWS):
        lo_c, hi_c = g * POOL_GROUP, (g + 1) * POOL_GROUP
        lo = jnp.maximum(t + 1 - w, 0)
        win_sum = cs[:, 1:, lo_c:hi_c] - cs[:, lo, lo_c:hi_c]
        cnt = jnp.minimum(t + 1, w).astype(jnp.float32)[None, :, None]
        groups.append(win_sum / cnt - uf[..., lo_c:hi_c])
    pooled = jnp.stack(groups, axis=2).astype(u.dtype)
    y = jnp.einsum('bsgc,gcd->bsgd', pooled, w_pool) + b_pool
    return y.reshape(B, S, C) * pool_scale


def moe_ffn(h, w_router, b_router, w_gate_up, b_gate_up, w_down, b_down):
    N, D = h.shape
    logits = (h @ w_router + b_router).astype(jnp.float32)
    top_v, top_i = lax.top_k(logits, TOP_K)
    gates = jax.nn.softmax(top_v, axis=-1)
    NK = N * TOP_K
    flat_e = top_i.reshape(-1).astype(jnp.int32)
    flat_tok = jnp.arange(NK, dtype=jnp.int32) // TOP_K
    flat_g = gates.reshape(-1)
    order = jnp.argsort(flat_e)
    sorted_e = flat_e[order]
    sorted_tok = flat_tok[order]
    sorted_g = flat_g[order]
    counts = jnp.bincount(flat_e, length=N_EXPERTS).astype(jnp.int32)
    padded = ((counts + EXPERT_BLOCK - 1) // EXPERT_BLOCK) * EXPERT_BLOCK
    pad_end = jnp.cumsum(padded)
    pad_start = pad_end - padded
    start = jnp.cumsum(counts) - counts
    rank = jnp.arange(NK, dtype=jnp.int32) - start[sorted_e]
    dest = pad_start[sorted_e] + rank
    n_blocks = -(-NK // EXPERT_BLOCK) + N_EXPERTS
    P = n_blocks * EXPERT_BLOCK
    row_tok = jnp.full((P,), N, jnp.int32).at[dest].set(sorted_tok)
    h_pad = jnp.concatenate([h, jnp.zeros((1, D), h.dtype)], axis=0)[row_tok]
    h_pad = h_pad.reshape(n_blocks, EXPERT_BLOCK, D)
    block_e = jnp.searchsorted(pad_end, jnp.arange(n_blocks, dtype=jnp.int32) * EXPERT_BLOCK, side='right')
    block_e = jnp.minimum(block_e, N_EXPERTS - 1).astype(jnp.int32)

    def expert_block(args):
        xb, e = args
        hgu = xb @ w_gate_up[e] + b_gate_up[e]
        gate = jnp.minimum(hgu[:, :D_FF], SWIGLU_LIMIT)
        up = jnp.clip(hgu[:, D_FF:], -SWIGLU_LIMIT, SWIGLU_LIMIT)
        act = (up + 1.0) * (gate * jax.nn.sigmoid(SWIGLU_ALPHA * gate))
        return act @ w_down[e] + b_down[e]

    y_pad = lax.map(expert_block, (h_pad, block_e)).reshape(P, D)
    y_sorted = y_pad[dest] * sorted_g[:, None].astype(y_pad.dtype)
    return jax.ops.segment_sum(y_sorted, sorted_tok, num_segments=N)


def setup_inputs(seed: int = 0) -> dict:
    key = jax.random.key(seed)
    ks = jax.random.split(key, 26)
    f32 = jnp.float32
    nrm = lambda k, shape, s: jax.random.normal(k, shape, f32) * s
    D, L, E, H = D_MODEL, DEPTH, N_EXPERTS, N_ATTN_HEADS
    return {
        "x": nrm(ks[0], (BATCH, SEQ, D), 1.0),
        "c": nrm(ks[1], (BATCH, D), 1.0),
        "w_ada": nrm(ks[2], (L, D, 6 * D), 0.5 * D ** -0.5),
        "b_ada": nrm(ks[3], (L, 6 * D), 0.01),
        "w_in": nrm(ks[4], (L, D, IN_COLS), D ** -0.5),
        "lambda_q1": nrm(ks[5], (L, DIFF_HEAD_DIM), 0.1),
        "lambda_k1": nrm(ks[6], (L, DIFF_HEAD_DIM), 0.1),
        "lambda_q2": nrm(ks[7], (L, DIFF_HEAD_DIM), 0.1),
        "lambda_k2": nrm(ks[8], (L, DIFF_HEAD_DIM), 0.1),
        "subln_g": 1.0 + nrm(ks[9], (L, 2 * DIFF_HEAD_DIM), 0.1),
        "rel_bias": nrm(ks[10], (N_BUCKETS, H), 0.5),
        "w_pool": nrm(ks[11], (L, N_POOL_GROUPS, POOL_GROUP, POOL_GROUP), POOL_GROUP ** -0.5),
        "b_pool": nrm(ks[12], (L, N_POOL_GROUPS, POOL_GROUP), 0.01),
        "pool_scale": 1.0 + nrm(ks[13], (L, POOL_WIDTH), 0.1),
        "w_out": nrm(ks[14], (L, D, D), D ** -0.5 * DEEPNORM_BETA),
        "ln1_g": 1.0 + nrm(ks[15], (L, D), 0.1),
        "ln1_b": nrm(ks[16], (L, D), 0.01),
        "w_router": nrm(ks[17], (L, D, E), D ** -0.5),
        "b_router": nrm(ks[18], (L, E), 0.01),
        "w_gate_up": nrm(ks[19], (L, E, D, 2 * D_FF), D ** -0.5),
        "b_gate_up": nrm(ks[20], (L, E, 2 * D_FF), 0.01),
        "w_down": nrm(ks[21], (L, E, D_FF, D), D_FF ** -0.5 * DEEPNORM_BETA),
        "b_down": nrm(ks[22], (L, E, D), 0.01),
        "ln2_g": 1.0 + nrm(ks[23], (L, D), 0.1),
        "ln2_b": nrm(ks[24], (L, D), 0.01),
    }


def reference(x, c, w_ada, b_ada, w_in, lambda_q1, lambda_k1, lambda_q2, lambda_k2, subln_g,
              rel_bias, w_pool, b_pool, pool_scale, w_out, ln1_g, ln1_b, w_router, b_router,
              w_gate_up, b_gate_up, w_down, b_down, ln2_g, ln2_b):
    B, S, D = x.shape
    A = ATTN_WIDTH
    cond = jax.nn.silu(c)
    for l in range(DEPTH):
        mod = cond @ w_ada[l] + b_ada[l]
        sh1, sc1, g1, sh2, sc2, g2 = jnp.split(mod, 6, axis=-1)

        h = modulate(layer_norm(x), sh1, sc1)
        proj = h @ w_in[l]
        q = proj[..., :A].reshape(B, S, N_ATTN_HEADS, 2, DIFF_HEAD_DIM)
        k = proj[..., A:2 * A].reshape(B, S, N_ATTN_HEADS, 2, DIFF_HEAD_DIM)
        v = proj[..., 2 * A:3 * A].reshape(B, S, N_ATTN_HEADS, 2 * DIFF_HEAD_DIM)
        u = proj[..., 3 * A:]
        lam_init = 0.8 - 0.6 * math.exp(-0.3 * l)
        lam = (jnp.exp(jnp.sum(lambda_q1[l].astype(jnp.float32) * lambda_k1[l].astype(jnp.float32)))
               - jnp.exp(jnp.sum(lambda_q2[l].astype(jnp.float32) * lambda_k2[l].astype(jnp.float32)))
               + lam_init)
        o = diff_attention(q, k, v, lam, rel_bias)
        o = rms_norm(o, subln_g[l]) * (1.0 - lam_init)
        p = pool_mixer(u, w_pool[l], b_pool[l], pool_scale[l])
        mix = jnp.concatenate([o.reshape(B, S, A), p], axis=-1) @ w_out[l]
        x = layer_norm(DEEPNORM_ALPHA * x + g1[:, None, :] * mix, ln1_g[l], ln1_b[l])

        h2 = modulate(layer_norm(x), sh2, sc2)
        y = moe_ffn(h2.reshape(B * S, D), w_router[l], b_router[l], w_gate_up[l], b_gate_up[l],
                    w_down[l], b_down[l]).reshape(B, S, D)
        x = layer_norm(DEEPNORM_ALPHA * x + g2[:, None, :] * y, ln2_g[l], ln2_b[l])
    return x
```

```python
import math

import jax
import jax.numpy as jnp
import numpy as np
from jax import lax
from jax.experimental import pallas as pl
from jax.experimental.pallas import tpu as pltpu

F32 = jnp.float32
BF16 = jnp.bfloat16
I32 = jnp.int32

D_MODEL = 1024
DEPTH = 1
CHUNK = 64
ATTN_WIDTH = 512
POOL_WIDTH = 512
HEAD_DIM = 64
N_HEADS = 4
POOL_WINDOWS = (2, 4, 8, 16)
POOL_GROUP = 128
IN_COLS = 3 * ATTN_WIDTH + POOL_WIDTH
N_BUCKETS = 32
MAX_DISTANCE = 128
N_EXPERTS = 32
TOP_K = 4
D_FF = D_MODEL
SWIGLU_LIMIT = 7.0
SWIGLU_ALPHA = 1.702
DEEPNORM_ALPHA = (2.0 * DEPTH) ** 0.25
LN_EPS = 1e-5
MASK_BIAS = -1e30

LANES = 128
SUBLANES = 8
VMEM_LIMIT = 56 * 1024 * 1024

ROW_TILE = 512
ATT_TILE = 256
TOK_TILE = 256
SEG_ROWS = 32
FFN_ROWS = 256
STAGE_ROWS = TOK_TILE * TOP_K + N_EXPERTS * SEG_ROWS
ROW_SLABS = D_MODEL // LANES


def _dot(a, b):
    return jnp.dot(a, b, preferred_element_type=F32)


def _dot_nt(a, b):
    return lax.dot_general(a, b, (((1,), (1,)), ((), ())), preferred_element_type=F32)


def _split(a):
    hi = a.astype(BF16)
    lo = (a - hi.astype(F32)).astype(BF16)
    return hi, lo


def _dot3(a, b):
    ah, al = _split(a)
    bh, bl = _split(b)
    return _dot(ah, bh) + (_dot(ah, bl) + _dot(al, bh))


def _dot3_nt(a, b):
    ah, al = _split(a)
    bh, bl = _split(b)
    return _dot_nt(ah, bh) + (_dot_nt(ah, bl) + _dot_nt(al, bh))


def _layer_norm(x):
    mu = jnp.mean(x, axis=-1, keepdims=True)
    xc = x - mu
    var = jnp.mean(xc * xc, axis=-1, keepdims=True)
    return xc * lax.rsqrt(var + LN_EPS)


def _ada_kernel(c_ref, w_ref, b_ref, o_ref):
    c = c_ref[...]
    cond = c * jax.nn.sigmoid(c)
    o_ref[...] = _dot3(cond, w_ref[...]) + b_ref[...]


def _ada(c, w_ada, b_ada):
    batch = c.shape[0]
    n_out = w_ada.shape[1]
    return pl.pallas_call(
        _ada_kernel,
        grid=(n_out // D_MODEL,),
        in_specs=[
            pl.BlockSpec((batch, D_MODEL), lambda n: (0, 0)),
            pl.BlockSpec((D_MODEL, D_MODEL), lambda n: (0, n)),
            pl.BlockSpec((1, D_MODEL), lambda n: (0, n)),
        ],
        out_specs=pl.BlockSpec((batch, D_MODEL), lambda n: (0, n)),
        out_shape=jax.ShapeDtypeStruct((batch, n_out), F32),
        compiler_params=pltpu.CompilerParams(vmem_limit_bytes=VMEM_LIMIT),
    )(c, w_ada, b_ada)


def _inproj_kernel(x_ref, mod_ref, w_ref, q_ref, k_ref, v_ref, u_ref):
    h = _layer_norm(x_ref[...]) * (1.0 + mod_ref[1:2, :]) + mod_ref[0:1, :]
    proj = _dot(h.astype(BF16), w_ref[...])
    a = ATTN_WIDTH
    q_ref[...] = (proj[:, :a] * (HEAD_DIM ** -0.5)).astype(BF16)
    k_ref[...] = proj[:, a:2 * a].astype(BF16)
    v_ref[...] = proj[:, 2 * a:3 * a].astype(BF16)
    u_ref[...] = proj[:, 3 * a:]


def _inproj(x2, mod3, w_in_bf, seq):
    n = x2.shape[0]
    tiles_per_batch = seq // ROW_TILE
    row = lambda r: (r, 0)
    return pl.pallas_call(
        _inproj_kernel,
        grid=(n // ROW_TILE,),
        in_specs=[
            pl.BlockSpec((ROW_TILE, D_MODEL), row),
            pl.BlockSpec((None, 6, D_MODEL), lambda r: (r // tiles_per_batch, 0, 0)),
            pl.BlockSpec((D_MODEL, IN_COLS), lambda r: (0, 0)),
        ],
        out_specs=[
            pl.BlockSpec((ROW_TILE, ATTN_WIDTH), row),
            pl.BlockSpec((ROW_TILE, ATTN_WIDTH), row),
            pl.BlockSpec((ROW_TILE, ATTN_WIDTH), row),
            pl.BlockSpec((ROW_TILE, POOL_WIDTH), row),
        ],
        out_shape=[
            jax.ShapeDtypeStruct((n, ATTN_WIDTH), BF16),
            jax.ShapeDtypeStruct((n, ATTN_WIDTH), BF16),
            jax.ShapeDtypeStruct((n, ATTN_WIDTH), BF16),
            jax.ShapeDtypeStruct((n, POOL_WIDTH), F32),
        ],
        compiler_params=pltpu.CompilerParams(vmem_limit_bytes=VMEM_LIMIT),
    )(x2, mod3, w_in_bf)


def _t5_bucket_np(rel):
    nb = N_BUCKETS // 2
    ret = (rel > 0).astype(np.int32) * nb
    n = np.abs(rel)
    max_exact = nb // 2
    nf = np.maximum(n, 1).astype(np.float32)
    large = max_exact + (np.log(nf / np.float32(max_exact))
                         / np.float32(math.log(MAX_DISTANCE / max_exact))
                         * np.float32(nb - max_exact)).astype(np.int32)
    large = np.minimum(large, nb - 1)
    return ret + np.where(n < max_exact, n, large)


def _bias_tiles(rel_bias, n_q):
    t = ATT_TILE
    qi = np.arange(t, dtype=np.int32)[:, None]
    kj = np.arange(t, dtype=np.int32)[None, :]
    buckets = np.stack([_t5_bucket_np(kj - qi - d * t) for d in range(n_q)])
    tiles = jnp.transpose(rel_bias.astype(F32)[buckets], (3, 0, 1, 2))
    allowed = (kj // CHUNK) <= (qi // CHUNK)
    mask = np.zeros((n_q, t, t), np.float32)
    mask[0] = np.where(allowed, 0.0, MASK_BIAS)
    return tiles + mask[None]


def _attn_kernel(lam_ref, q_ref, k_ref, v_ref, bias_ref, g_ref, o_ref):
    i = pl.program_id(2)
    t = ATT_TILE
    q = q_ref[...]
    q_maps = (q[:, :HEAD_DIM], q[:, HEAD_DIM:])

    def body(j, carry):
        start = pl.multiple_of(j * t, t)
        ks = k_ref[pl.ds(start, t), :]
        vs = v_ref[pl.ds(start, t), :]
        bias = bias_ref[i - j]
        k_maps = (ks[:, :HEAD_DIM], ks[:, HEAD_DIM:])
        out = []
        for mi in range(2):
            m, l, acc = carry[mi]
            s = _dot_nt(q_maps[mi], k_maps[mi]) + bias
            m_new = jnp.maximum(m, jnp.max(s, axis=-1, keepdims=True))
            alpha = jnp.exp(m - m_new)
            p = jnp.exp(s - m_new)
            l = alpha * l + jnp.sum(p, axis=-1, keepdims=True)
            acc = alpha * acc + _dot(p.astype(BF16), vs)
            out.append((m_new, l, acc))
        return tuple(out)

    def init():
        return (jnp.full((t, 1), MASK_BIAS, F32), jnp.zeros((t, 1), F32),
                jnp.zeros((t, 2 * HEAD_DIM), F32))

    (_, l0, a0), (_, l1, a1) = lax.fori_loop(0, i + 1, body, (init(), init()))
    o = a0 / l0 - lam_ref[0] * (a1 / l1)
    lam_init = 0.8 - 0.6 * math.exp(-0.3 * 0)
    y = o * lax.rsqrt(jnp.mean(o * o, axis=-1, keepdims=True) + LN_EPS)
    o_ref[...] = ((y * g_ref[...]) * (1.0 - lam_init)).astype(BF16)


def _attention(lam, q, k, v, bias_tiles, subln_g, batch, seq):
    n = q.shape[0]
    n_q = seq // ATT_TILE
    hw = 2 * HEAD_DIM
    return pl.pallas_call(
        _attn_kernel,
        grid=(batch, N_HEADS, n_q),
        in_specs=[
            pl.BlockSpec(memory_space=pltpu.SMEM),
            pl.BlockSpec((ATT_TILE, hw), lambda b, h, i: (b * n_q + i, h)),
            pl.BlockSpec((seq, hw), lambda b, h, i: (b, h)),
            pl.BlockSpec((seq, hw), lambda b, h, i: (b, h)),
            pl.BlockSpec((None, n_q, ATT_TILE, ATT_TILE), lambda b, h, i: (h, 0, 0, 0)),
            pl.BlockSpec((1, hw), lambda b, h, i: (0, 0)),
        ],
        out_specs=pl.BlockSpec((ATT_TILE, hw), lambda b, h, i: (b * n_q + i, h)),
        out_shape=jax.ShapeDtypeStruct((n, ATTN_WIDTH), BF16),
        compiler_params=pltpu.CompilerParams(vmem_limit_bytes=VMEM_LIMIT),
    )(lam, q, k, v, bias_tiles, subln_g)


def _shift_rows(x, d):
    rolled = pltpu.roll(x, d, axis=0)
    t = lax.broadcasted_iota(I32, x.shape, 0)
    return jnp.where(t >= d, rolled, 0.0)


def _pool_kernel(u_ref, w_ref, b_ref, scale_ref, p_ref):
    seq = u_ref.shape[0]
    t = lax.broadcasted_iota(I32, (seq, POOL_GROUP), 0)
    for g, window in enumerate(POOL_WINDOWS):
        cols = slice(g * POOL_GROUP, (g + 1) * POOL_GROUP)
        u = u_ref[:, cols]
        win_sum = u
        span = 1
        while span < window:
            win_sum = win_sum + _shift_rows(win_sum, span)
            span *= 2
        cnt = jnp.minimum(t + 1, window).astype(F32)
        pooled = win_sum / cnt - u
        y = _dot(pooled.astype(BF16), w_ref[g]) + b_ref[g]
        p_ref[:, cols] = (y * scale_ref[:, cols]).astype(BF16)


def _pool(u, w_pool_bf, b_pool, pool_scale, batch, seq):
    n = u.shape[0]
    n_groups = len(POOL_WINDOWS)
    return pl.pallas_call(
        _pool_kernel,
        grid=(batch,),
        in_specs=[
            pl.BlockSpec((seq, POOL_WIDTH), lambda b: (b, 0)),
            pl.BlockSpec((n_groups, POOL_GROUP, POOL_GROUP), lambda b: (0, 0, 0)),
            pl.BlockSpec((n_groups, 1, POOL_GROUP), lambda b: (0, 0, 0)),
            pl.BlockSpec((1, POOL_WIDTH), lambda b: (0, 0)),
        ],
        out_specs=pl.BlockSpec((seq, POOL_WIDTH), lambda b: (b, 0)),
        out_shape=jax.ShapeDtypeStruct((n, POOL_WIDTH), BF16),
        compiler_params=pltpu.CompilerParams(vmem_limit_bytes=VMEM_LIMIT),
    )(u, w_pool_bf, b_pool, pool_scale)


def _outproj_kernel(o_ref, p_ref, x_ref, mod_ref, wo_ref, g1_ref, b1_ref, wrt_ref, br_ref,
                    x1_ref, h2_ref, pos_ref, gate_ref, cnt_ref):
    tm = TOK_TILE
    ne = N_EXPERTS
    mix = _dot(o_ref[...], wo_ref[:ATTN_WIDTH, :]) + _dot(p_ref[...], wo_ref[ATTN_WIDTH:, :])
    x1 = _layer_norm(DEEPNORM_ALPHA * x_ref[...] + mod_ref[2:3, :] * mix) * g1_ref[...] + b1_ref[...]
    x1_ref[...] = x1
    h2 = _layer_norm(x1) * (1.0 + mod_ref[4:5, :]) + mod_ref[3:4, :]
    h2_ref[...] = h2.astype(BF16)

    logits = _dot3_nt(wrt_ref[...], h2) + br_ref[...]
    e_idx = lax.broadcasted_iota(I32, (ne, tm), 0).astype(F32)
    top_v, sels = [], []
    for _ in range(TOP_K):
        mx = jnp.max(logits, axis=0, keepdims=True)
        pick = jnp.min(jnp.where(logits == mx, e_idx, float(ne)), axis=0, keepdims=True)
        sel = e_idx == pick
        logits = jnp.where(sel, -jnp.inf, logits)
        top_v.append(mx)
        sels.append(sel)
    exps = [jnp.exp(v - top_v[0]) for v in top_v]
    denom = exps[0] + exps[1] + exps[2] + exps[3]
    gate_ref[...] = jnp.concatenate([e / denom for e in exps], axis=0)

    onehot = jnp.zeros((ne, tm), F32)
    for sel in sels:
        onehot = onehot + jnp.where(sel, 1.0, 0.0)
    earlier = (lax.broadcasted_iota(I32, (tm, tm), 0) < lax.broadcasted_iota(I32, (tm, tm), 1))
    rank = _dot(onehot.astype(BF16), jnp.where(earlier, 1.0, 0.0).astype(BF16))
    cnt = jnp.sum(onehot, axis=1, keepdims=True)
    padded = jnp.floor((cnt + (SEG_ROWS - 1)) * (1.0 / SEG_ROWS)) * SEG_ROWS
    lower = (lax.broadcasted_iota(I32, (ne, ne), 1) < lax.broadcasted_iota(I32, (ne, ne), 0))
    seg_off = _dot(jnp.where(lower, 1.0, 0.0).astype(BF16),
                   jnp.broadcast_to(padded, (ne, tm)).astype(BF16))
    slot = seg_off + rank
    pos = [jnp.sum(jnp.where(sel, slot, 0.0), axis=0, keepdims=True) for sel in sels]
    pos_ref[...] = jnp.concatenate(pos, axis=0).astype(I32)
    cnt_ref[...] = jnp.broadcast_to(cnt, (ne, LANES)).astype(I32)


def _outproj(o, p, x2, mod3, w_out_bf, ln1_g, ln1_b, w_router_t, b_router_col, seq):
    n = x2.shape[0]
    tiles_per_batch = seq // TOK_TILE
    n_tiles = n // TOK_TILE
    row = lambda r: (r, 0)
    const = lambda r: (0, 0)
    return pl.pallas_call(
        _outproj_kernel,
        grid=(n_tiles,),
        in_specs=[
            pl.BlockSpec((TOK_TILE, ATTN_WIDTH), row),
            pl.BlockSpec((TOK_TILE, POOL_WIDTH), row),
            pl.BlockSpec((TOK_TILE, D_MODEL), row),
            pl.BlockSpec((None, 6, D_MODEL), lambda r: (r // tiles_per_batch, 0, 0)),
            pl.BlockSpec((D_MODEL, D_MODEL), const),
            pl.BlockSpec((1, D_MODEL), const),
            pl.BlockSpec((1, D_MODEL), const),
            pl.BlockSpec((N_EXPERTS, D_MODEL), const),
            pl.BlockSpec((N_EXPERTS, 1), const),
        ],
        out_specs=[
            pl.BlockSpec((TOK_TILE, D_MODEL), row),
            pl.BlockSpec((TOK_TILE, D_MODEL), row),
            pl.BlockSpec((TOP_K, TOK_TILE), lambda r: (0, r)),
            pl.BlockSpec((TOP_K, TOK_TILE), lambda r: (0, r)),
            pl.BlockSpec((None, N_EXPERTS, LANES), lambda r: (r, 0, 0)),
        ],
        out_shape=[
            jax.ShapeDtypeStruct((n, D_MODEL), F32),
            jax.ShapeDtypeStruct((n, D_MODEL), BF16),
            jax.ShapeDtypeStruct((TOP_K, n), I32),
            jax.ShapeDtypeStruct((TOP_K, n), F32),
            jax.ShapeDtypeStruct((n_tiles, N_EXPERTS, LANES), I32),
        ],
        compiler_params=pltpu.CompilerParams(vmem_limit_bytes=VMEM_LIMIT),
    )(o, p, x2, mod3, w_out_bf, ln1_g, ln1_b, w_router_t, b_router_col)


def _segment_copies(tile, nch_ref, make_copy, op):
    def per_expert(e, carry):
        idx = tile * N_EXPERTS + e

        def per_chunk(c, carry2):
            copy = make_copy(idx, c)
            copy.start() if op == "start" else copy.wait()
            return carry2

        return lax.fori_loop(0, nch_ref[idx], per_chunk, carry)

    lax.fori_loop(0, N_EXPERTS, per_expert, 0)


def _dispatch_kernel(base_ref, nch_ref, off_ref, ztail_ref, nz_ref, h2_ref, pos_ref, xs_ref,
                     stage_ref, zero_ref, sem):
    tile = pl.program_id(0)
    seg = SEG_ROWS * ROW_SLABS

    @pl.when(tile == 0)
    def _():
        zero_ref[...] = jnp.zeros(zero_ref.shape, F32)

        def zero_copy(e, c):
            dst = pl.multiple_of((ztail_ref[e] + c * SEG_ROWS) * ROW_SLABS, ROW_SLABS)
            return pltpu.make_async_copy(zero_ref, xs_ref.at[pl.ds(dst, seg)], sem)

        for lo, hi in ((0, N_EXPERTS), (N_EXPERTS, N_EXPERTS + 1)):
            for op in ("start", "wait"):
                def per_expert(e, carry, op=op):
                    def per_chunk(c, carry2):
                        copy = zero_copy(e, c)
                        copy.start() if op == "start" else copy.wait()
                        return carry2
                    return lax.fori_loop(0, nz_ref[e], per_chunk, carry)
                lax.fori_loop(lo, hi, per_expert, 0)

    pos = pos_ref[...]
    r = lax.broadcasted_iota(I32, (STAGE_ROWS, TOK_TILE), 0)
    perm = jnp.where(r == pos[0:1], 1.0,
                     jnp.where(r == pos[1:2], 1.0,
                               jnp.where(r == pos[2:3], 1.0,
                                         jnp.where(r == pos[3:4], 1.0, 0.0))))
    sorted_rows = _dot(perm.astype(BF16), h2_ref[...])
    for s in range(ROW_SLABS):
        stage_ref[pl.ds(s, STAGE_ROWS, stride=ROW_SLABS), :] = sorted_rows[:, s * LANES:(s + 1) * LANES]

    def row_copy(idx, c):
        src = pl.multiple_of((off_ref[idx] + c * SEG_ROWS) * ROW_SLABS, ROW_SLABS)
        dst = pl.multiple_of((base_ref[idx] + c * SEG_ROWS) * ROW_SLABS, ROW_SLABS)
        return pltpu.make_async_copy(stage_ref.at[pl.ds(src, seg)], xs_ref.at[pl.ds(dst, seg)], sem)

    _segment_copies(tile, nch_ref, row_copy, "start")
    _segment_copies(tile, nch_ref, row_copy, "wait")


def _dispatch(base, nch, off, ztail, nz, h2, pos, n_rows):
    n = h2.shape[0]
    return pl.pallas_call(
        _dispatch_kernel,
        grid_spec=pltpu.PrefetchScalarGridSpec(
            num_scalar_prefetch=5,
            grid=(n // TOK_TILE,),
            in_specs=[
                pl.BlockSpec((TOK_TILE, D_MODEL), lambda t, *_: (t, 0)),
                pl.BlockSpec((TOP_K, TOK_TILE), lambda t, *_: (0, t)),
            ],
            out_specs=pl.BlockSpec(memory_space=pl.ANY),
            scratch_shapes=[
                pltpu.VMEM((STAGE_ROWS * ROW_SLABS, LANES), F32),
                pltpu.VMEM((SEG_ROWS * ROW_SLABS, LANES), F32),
                pltpu.SemaphoreType.DMA(()),
            ],
        ),
        out_shape=jax.ShapeDtypeStruct((n_rows * ROW_SLABS, LANES), F32),
        compiler_params=pltpu.CompilerParams(vmem_limit_bytes=VMEM_LIMIT),
    )(base, nch, off, ztail, nz, h2, pos)


def _ffn_kernel(be_ref, bidx_ref, first_ref, valid_ref, xs_ref, wgu_ref, bgu_ref, wd_ref, bd_ref,
                ys_ref, wgu_bf, wd_bf):
    j = pl.program_id(0)

    @pl.when(valid_ref[j] == 0)
    def _():
        ys_ref[...] = jnp.zeros(ys_ref.shape, F32)

    @pl.when(valid_ref[j] == 1)
    def _():
        @pl.when(first_ref[j] == 1)
        def _():
            wgu_bf[...] = wgu_ref[...].astype(BF16)
            wd_bf[...] = wd_ref[...].astype(BF16)

        x = jnp.concatenate(
            [xs_ref[pl.ds(s, FFN_ROWS, stride=ROW_SLABS), :] for s in range(ROW_SLABS)], axis=-1)
        hgu = _dot(x.astype(BF16), wgu_bf[...]) + bgu_ref[...]
        gate = jnp.minimum(hgu[:, :D_FF], SWIGLU_LIMIT)
        up = jnp.clip(hgu[:, D_FF:], -SWIGLU_LIMIT, SWIGLU_LIMIT)
        act = (up + 1.0) * (gate * jax.nn.sigmoid(SWIGLU_ALPHA * gate))
        y = _dot(act.astype(BF16), wd_bf[...]) + bd_ref[...]
        for s in range(ROW_SLABS):
            ys_ref[pl.ds(s, FFN_ROWS, stride=ROW_SLABS), :] = y[:, s * LANES:(s + 1) * LANES]


def _ffn(block_e, block_idx, block_first, block_valid, xs, w_gate_up, b_gate_up, w_down, b_down):
    n_blocks = block_e.shape[0]
    blk = FFN_ROWS * ROW_SLABS
    expert = lambda j, be, bi, bf, bv: (be[j], 0, 0)
    rows = lambda j, be, bi, bf, bv: (bi[j], 0)
    return pl.pallas_call(
        _ffn_kernel,
        grid_spec=pltpu.PrefetchScalarGridSpec(
            num_scalar_prefetch=4,
            grid=(n_blocks,),
            in_specs=[
                pl.BlockSpec((blk, LANES), rows),
                pl.BlockSpec((None, D_MODEL, 2 * D_FF), expert),
                pl.BlockSpec((None, 1, 2 * D_FF), expert),
                pl.BlockSpec((None, D_FF, D_MODEL), expert),
                pl.BlockSpec((None, 1, D_MODEL), expert),
            ],
            out_specs=pl.BlockSpec((blk, LANES), lambda j, be, bi, bf, bv: (j, 0)),
            scratch_shapes=[
                pltpu.VMEM((D_MODEL, 2 * D_FF), BF16),
                pltpu.VMEM((D_FF, D_MODEL), BF16),
            ],
        ),
        out_shape=jax.ShapeDtypeStruct(xs.shape, F32),
        compiler_params=pltpu.CompilerParams(vmem_limit_bytes=VMEM_LIMIT),
    )(block_e, block_idx, block_first, block_valid, xs, w_gate_up, b_gate_up, w_down, b_down)


def _combine_kernel(base_ref, nch_ref, off_ref, ys_ref, pos_ref, gate_ref, x1_ref, mod_ref,
                    g2_ref, b2_ref, out_ref, stage_ref, sem):
    tile = pl.program_id(0)
    seg = SEG_ROWS * ROW_SLABS

    @pl.when(tile == 0)
    def _():
        stage_ref[...] = jnp.zeros(stage_ref.shape, F32)

    def row_copy(idx, c):
        src = pl.multiple_of((base_ref[idx] + c * SEG_ROWS) * ROW_SLABS, ROW_SLABS)
        dst = pl.multiple_of((off_ref[idx] + c * SEG_ROWS) * ROW_SLABS, ROW_SLABS)
        return pltpu.make_async_copy(ys_ref.at[pl.ds(src, seg)], stage_ref.at[pl.ds(dst, seg)], sem)

    _segment_copies(tile, nch_ref, row_copy, "start")
    _segment_copies(tile, nch_ref, row_copy, "wait")

    y_sorted = jnp.concatenate(
        [stage_ref[pl.ds(s, STAGE_ROWS, stride=ROW_SLABS), :] for s in range(ROW_SLABS)], axis=-1)
    pos = pos_ref[...]
    gate = gate_ref[...]
    r = lax.broadcasted_iota(I32, (TOK_TILE, STAGE_ROWS), 1)
    weights = jnp.where(r == pos[:, 0:1], gate[:, 0:1],
                        jnp.where(r == pos[:, 1:2], gate[:, 1:2],
                                  jnp.where(r == pos[:, 2:3], gate[:, 2:3],
                                            jnp.where(r == pos[:, 3:4], gate[:, 3:4], 0.0))))
    y = _dot(weights.astype(BF16), y_sorted.astype(BF16))
    z = DEEPNORM_ALPHA * x1_ref[...] + mod_ref[5:6, :] * y
    out_ref[...] = _layer_norm(z) * g2_ref[...] + b2_ref[...]


def _combine(base, nch, off, ys, pos_t, gate_t, x1, mod3, ln2_g, ln2_b, seq):
    n = x1.shape[0]
    tiles_per_batch = seq // TOK_TILE
    row = lambda t, *_: (t, 0)
    const = lambda t, *_: (0, 0)
    return pl.pallas_call(
        _combine_kernel,
        grid_spec=pltpu.PrefetchScalarGridSpec(
            num_scalar_prefetch=3,
            grid=(n // TOK_TILE,),
            in_specs=[
                pl.BlockSpec(memory_space=pl.ANY),
                pl.BlockSpec((TOK_TILE, TOP_K), row),
                pl.BlockSpec((TOK_TILE, TOP_K), row),
                pl.BlockSpec((TOK_TILE, D_MODEL), row),
                pl.BlockSpec((None, 6, D_MODEL), lambda t, *_: (t // tiles_per_batch, 0, 0)),
                pl.BlockSpec((1, D_MODEL), const),
                pl.BlockSpec((1, D_MODEL), const),
            ],
            out_specs=pl.BlockSpec((TOK_TILE, D_MODEL), row),
            scratch_shapes=[
                pltpu.VMEM((STAGE_ROWS * ROW_SLABS, LANES), F32),
                pltpu.SemaphoreType.DMA(()),
            ],
        ),
        out_shape=jax.ShapeDtypeStruct((n, D_MODEL), F32),
        compiler_params=pltpu.CompilerParams(vmem_limit_bytes=VMEM_LIMIT),
    )(base, nch, off, ys, pos_t, gate_t, x1, mod3, ln2_g, ln2_b)


def _ceil_div(a, b):
    return (a + b - 1) // b


def _routing_plan(cnt, n_blocks):
    counts = jnp.sum(cnt, axis=0)
    region = _ceil_div(counts + (SEG_ROWS - 1), FFN_ROWS) * FFN_ROWS
    rend = jnp.cumsum(region)
    rstart = rend - region
    base = rstart[None, :] + (jnp.cumsum(cnt, axis=0) - cnt)
    nch = _ceil_div(cnt, SEG_ROWS)
    padded = nch * SEG_ROWS
    off = jnp.cumsum(padded, axis=1) - padded
    n_rows = n_blocks * FFN_ROWS
    ztail = jnp.concatenate([rstart + counts, rend[-1:]])
    nz = jnp.concatenate([_ceil_div(rend - ztail[:-1], SEG_ROWS), (n_rows - rend[-1:]) // SEG_ROWS])
    blk_start = jnp.arange(n_blocks, dtype=I32) * FFN_ROWS
    block_e = jnp.minimum(jnp.searchsorted(rend, blk_start, side="right"), N_EXPERTS - 1).astype(I32)
    block_valid = (blk_start < rend[-1]).astype(I32)
    last_valid = rend[-1] // FFN_ROWS - 1
    block_idx = jnp.minimum(jnp.arange(n_blocks, dtype=I32), last_valid).astype(I32)
    block_first = (blk_start == rstart[block_e]).astype(I32)
    flat = lambda a: a.reshape(-1).astype(I32)
    return (flat(base), flat(nch), flat(off), flat(ztail), flat(nz),
            block_e, block_idx, block_first, block_valid)


def kernel(x, c, w_ada, b_ada, w_in, lambda_q1, lambda_k1, lambda_q2, lambda_k2, subln_g, rel_bias,
           w_pool, b_pool, pool_scale, w_out, ln1_g, ln1_b, w_router, b_router, w_gate_up, b_gate_up,
           w_down, b_down, ln2_g, ln2_b):
    batch, seq, d = x.shape
    n = batch * seq
    assert d == D_MODEL and w_ada.shape[0] == DEPTH == 1
    assert seq % ROW_TILE == 0 and seq % ATT_TILE == 0 and seq % TOK_TILE == 0
    x2 = x.reshape(n, d)

    mod3 = _ada(c, w_ada[0], b_ada).reshape(batch, 6, d)

    q, k, v, u = _inproj(x2, mod3, w_in[0].astype(BF16), seq)

    lam_init = 0.8 - 0.6 * math.exp(-0.3 * 0)
    lam = (jnp.exp(jnp.sum(lambda_q1[0].astype(F32) * lambda_k1[0].astype(F32)))
           - jnp.exp(jnp.sum(lambda_q2[0].astype(F32) * lambda_k2[0].astype(F32))) + lam_init)
    bias_tiles = _bias_tiles(rel_bias, seq // ATT_TILE)
    o = _attention(lam.reshape(1), q, k, v, bias_tiles, subln_g, batch, seq)

    p = _pool(u, w_pool[0].astype(BF16), b_pool[0][:, None, :], pool_scale, batch, seq)

    x1, h2, pos, gates, cnt_l = _outproj(
        o, p, x2, mod3, w_out[0].astype(BF16), ln1_g, ln1_b,
        w_router[0].T, b_router[0][:, None], seq)

    n_assign = n * TOP_K
    n_blocks = _ceil_div(n_assign + N_EXPERTS * (SEG_ROWS - 1 + FFN_ROWS - 1) + SEG_ROWS, FFN_ROWS)
    (base, nch, off, ztail, nz, block_e, block_idx, block_first, block_valid) = _routing_plan(
        cnt_l[:, :, 0], n_blocks)

    xs = _dispatch(base, nch, off, ztail, nz, h2, pos, n_blocks * FFN_ROWS)
    ys = _ffn(block_e, block_idx, block_first, block_valid, xs,
              w_gate_up[0], b_gate_up[0][:, None, :], w_down[0], b_down[0][:, None, :])
    out = _combine(base, nch, off, ys, pos.T, gates.T, x1, mod3, ln2_g, ln2_b, seq)
    return out.reshape(batch, seq, d)
```

```python
import functools
import math

import jax
import jax.numpy as jnp
import numpy as np
from jax import lax
from jax.experimental import pallas as pl
from jax.experimental.pallas import tpu as pltpu

F32 = jnp.float32
BF16 = jnp.bfloat16
I32 = jnp.int32

D_MODEL = 1024
DEPTH = 1
CHUNK = 64
ATTN_WIDTH = 512
POOL_WIDTH = 512
HEAD_DIM = 64
N_HEADS = 4
POOL_WINDOWS = (2, 4, 8, 16)
POOL_GROUP = 128
IN_COLS = 3 * ATTN_WIDTH + POOL_WIDTH
N_BUCKETS = 32
MAX_DISTANCE = 128
N_EXPERTS = 32
TOP_K = 4
D_FF = D_MODEL
SWIGLU_LIMIT = 7.0
SWIGLU_ALPHA = 1.702
DEEPNORM_ALPHA = (2.0 * DEPTH) ** 0.25
LN_EPS = 1e-5
MASK_BIAS = -1e30

LANES = 128
SUBLANES = 8
VMEM_LIMIT = 56 * 1024 * 1024

ROW_TILE = 512
ATT_TILE = 256
TOK_TILE = 256
SEG_ROWS = 32
FFN_ROWS = 256
STAGE_ROWS = TOK_TILE * TOP_K + N_EXPERTS * SEG_ROWS
ROW_SLABS = D_MODEL // LANES


def _dot(a, b):
    return jnp.dot(a, b, preferred_element_type=F32)


def _dot_nt(a, b):
    return lax.dot_general(a, b, (((1,), (1,)), ((), ())), preferred_element_type=F32)


def _split(a):
    hi = a.astype(BF16)
    lo = (a - hi.astype(F32)).astype(BF16)
    return hi, lo


def _dot3(a, b):
    ah, al = _split(a)
    bh, bl = _split(b)
    return _dot(ah, bh) + (_dot(ah, bl) + _dot(al, bh))


def _dot3_nt(a, b):
    ah, al = _split(a)
    bh, bl = _split(b)
    return _dot_nt(ah, bh) + (_dot_nt(ah, bl) + _dot_nt(al, bh))


def _layer_norm(x):
    mu = jnp.mean(x, axis=-1, keepdims=True)
    xc = x - mu
    var = jnp.mean(xc * xc, axis=-1, keepdims=True)
    return xc * lax.rsqrt(var + LN_EPS)


def _ada_kernel(c_ref, w_ref, b_ref, o_ref):
    c = c_ref[...]
    cond = c * jax.nn.sigmoid(c)
    o_ref[...] = _dot3(cond, w_ref[...]) + b_ref[...]


def _ada(c, w_ada, b_ada):
    batch = c.shape[0]
    n_out = w_ada.shape[1]
    return pl.pallas_call(
        _ada_kernel,
        grid=(n_out // D_MODEL,),
        in_specs=[
            pl.BlockSpec((batch, D_MODEL), lambda n: (0, 0)),
            pl.BlockSpec((D_MODEL, D_MODEL), lambda n: (0, n)),
            pl.BlockSpec((1, D_MODEL), lambda n: (0, n)),
        ],
        out_specs=pl.BlockSpec((batch, D_MODEL), lambda n: (0, n)),
        out_shape=jax.ShapeDtypeStruct((batch, n_out), F32),
        compiler_params=pltpu.CompilerParams(vmem_limit_bytes=VMEM_LIMIT),
    )(c, w_ada, b_ada)


def _inproj_kernel(x_ref, mod_ref, w_ref, q_ref, k_ref, v_ref, u_ref):
    h = _layer_norm(x_ref[...]) * (1.0 + mod_ref[1:2, :]) + mod_ref[0:1, :]
    proj = _dot(h.astype(BF16), w_ref[...])
    a = ATTN_WIDTH
    q_ref[...] = (proj[:, :a] * (HEAD_DIM ** -0.5)).astype(BF16)
    k_ref[...] = proj[:, a:2 * a].astype(BF16)
    v_ref[...] = proj[:, 2 * a:3 * a].astype(BF16)
    u_ref[...] = proj[:, 3 * a:]


def _inproj(x2, mod3, w_in_bf, seq):
    n = x2.shape[0]
    tiles_per_batch = seq // ROW_TILE
    row = lambda r: (r, 0)
    return pl.pallas_call(
        _inproj_kernel,
        grid=(n // ROW_TILE,),
        in_specs=[
            pl.BlockSpec((ROW_TILE, D_MODEL), row),
            pl.BlockSpec((None, 6, D_MODEL), lambda r: (r // tiles_per_batch, 0, 0)),
            pl.BlockSpec((D_MODEL, IN_COLS), lambda r: (0, 0)),
        ],
        out_specs=[
            pl.BlockSpec((ROW_TILE, ATTN_WIDTH), row),
            pl.BlockSpec((ROW_TILE, ATTN_WIDTH), row),
            pl.BlockSpec((ROW_TILE, ATTN_WIDTH), row),
            pl.BlockSpec((ROW_TILE, POOL_WIDTH), row),
        ],
        out_shape=[
            jax.ShapeDtypeStruct((n, ATTN_WIDTH), BF16),
            jax.ShapeDtypeStruct((n, ATTN_WIDTH), BF16),
            jax.ShapeDtypeStruct((n, ATTN_WIDTH), BF16),
            jax.ShapeDtypeStruct((n, POOL_WIDTH), F32),
        ],
        compiler_params=pltpu.CompilerParams(vmem_limit_bytes=VMEM_LIMIT),
    )(x2, mod3, w_in_bf)


def _t5_bucket_np(rel):
    nb = N_BUCKETS // 2
    ret = (rel > 0).astype(np.int32) * nb
    n = np.abs(rel)
    max_exact = nb // 2
    nf = np.maximum(n, 1).astype(np.float32)
    large = max_exact + (np.log(nf / np.float32(max_exact))
                         / np.float32(math.log(MAX_DISTANCE / max_exact))
                         * np.float32(nb - max_exact)).astype(np.int32)
    large = np.minimum(large, nb - 1)
    return ret + np.where(n < max_exact, n, large)


MASKED_BUCKET = N_BUCKETS


def _bucket_tiles(n_q):
    t = ATT_TILE
    qi = np.arange(t, dtype=np.int32)[:, None]
    kj = np.arange(t, dtype=np.int32)[None, :]
    tiles = np.stack([_t5_bucket_np(kj - qi - d * t) for d in range(n_q)]).astype(np.int32)
    allowed = (kj // CHUNK) <= (qi // CHUNK)
    tiles[0] = np.where(allowed, tiles[0], MASKED_BUCKET)
    uniq, ids = [], []
    for d in range(n_q):
        for u, tile in enumerate(uniq):
            if np.array_equal(tile, tiles[d]):
                ids.append(u)
                break
        else:
            ids.append(len(uniq))
            uniq.append(tiles[d])
    return np.stack(uniq), tuple(ids)


def _bias_kernel(rel_ref, bucket_ref, o_ref):
    h = pl.program_id(0)
    bucket = bucket_ref[...]
    acc = jnp.full(bucket.shape, MASK_BIAS, F32)
    for b in range(N_BUCKETS):
        acc = jnp.where(bucket == b, rel_ref[b, h], acc)
    o_ref[...] = acc


def _bias_tiles(rel_bias, bucket_tiles):
    n_u, t, _ = bucket_tiles.shape
    return pl.pallas_call(
        _bias_kernel,
        grid=(N_HEADS, n_u),
        in_specs=[
            pl.BlockSpec(memory_space=pltpu.SMEM),
            pl.BlockSpec((None, t, t), lambda h, u: (u, 0, 0)),
        ],
        out_specs=pl.BlockSpec((None, None, t, t), lambda h, u: (h, u, 0, 0)),
        out_shape=jax.ShapeDtypeStruct((N_HEADS, n_u, t, t), F32),
    )(rel_bias.astype(F32), jnp.asarray(bucket_tiles))


def _attn_tile(n_keys, tile_ids, lam_ref, q_ref, k_ref, v_ref, bias_ref, g_ref, o_ref):
    t = ATT_TILE
    q = q_ref[...]
    lane = lax.broadcasted_iota(I32, q.shape, 1)
    zero = jnp.zeros_like(q)
    q_both = jnp.concatenate([jnp.where(lane < HEAD_DIM, q, zero),
                              jnp.where(lane >= HEAD_DIM, q, zero)], axis=0)
    scores = []
    for j in range(n_keys):
        bias = bias_ref[tile_ids[n_keys - 1 - j]]
        s = _dot_nt(q_both, k_ref[j * t:(j + 1) * t, :])
        scores.append(s + jnp.concatenate([bias, bias], axis=0))
    m = scores[0]
    for s in scores[1:]:
        m = jnp.maximum(m, s)
    m = jnp.max(m, axis=-1, keepdims=True)
    probs = [jnp.exp(s - m) for s in scores]
    l = probs[0]
    for p in probs[1:]:
        l = l + p
    l = jnp.sum(l, axis=-1, keepdims=True)
    pv = _dot(jnp.concatenate([p.astype(BF16) for p in probs], axis=1), v_ref[:n_keys * t, :])
    pv = pv / l
    o = pv[:t] - lam_ref[0] * pv[t:]
    lam_init = 0.8 - 0.6 * math.exp(-0.3 * 0)
    y = o * lax.rsqrt(jnp.mean(o * o, axis=-1, keepdims=True) + LN_EPS)
    o_ref[...] = ((y * g_ref[...]) * (1.0 - lam_init)).astype(BF16)


def _attn_kernel(tile_ids, lam_ref, q_lo_ref, q_hi_ref, k_ref, v_ref, bias_ref, g_ref,
                 o_lo_ref, o_hi_ref):
    n_q = len(tile_ids)
    p = pl.program_id(2)
    for c in range(n_q // 2):
        @pl.when(p == c)
        def _(c=c):
            _attn_tile(c + 1, tile_ids, lam_ref, q_lo_ref, k_ref, v_ref, bias_ref, g_ref, o_lo_ref)
            _attn_tile(n_q - c, tile_ids, lam_ref, q_hi_ref, k_ref, v_ref, bias_ref, g_ref, o_hi_ref)


def _attention(lam, q, k, v, bias_tiles, tile_ids, subln_g, batch, seq):
    n = q.shape[0]
    n_q = seq // ATT_TILE
    half = n_q // 2
    n_u = bias_tiles.shape[1]
    hw = 2 * HEAD_DIM
    out_spec = pl.BlockSpec((ATT_TILE, hw), lambda b, h, p: (b * half + p, h))
    out_shape = jax.ShapeDtypeStruct((n // 2, ATTN_WIDTH), BF16)
    return pl.pallas_call(
        functools.partial(_attn_kernel, tile_ids),
        grid=(batch, N_HEADS, half),
        in_specs=[
            pl.BlockSpec(memory_space=pltpu.SMEM),
            pl.BlockSpec((ATT_TILE, hw), lambda b, h, p: (b * n_q + p, h)),
            pl.BlockSpec((ATT_TILE, hw), lambda b, h, p: (b * n_q + n_q - 1 - p, h)),
            pl.BlockSpec((seq, hw), lambda b, h, p: (b, h)),
            pl.BlockSpec((seq, hw), lambda b, h, p: (b, h)),
            pl.BlockSpec((None, n_u, ATT_TILE, ATT_TILE), lambda b, h, p: (h, 0, 0, 0)),
            pl.BlockSpec((1, hw), lambda b, h, p: (0, 0)),
        ],
        out_specs=[out_spec, out_spec],
        out_shape=[out_shape, out_shape],
        compiler_params=pltpu.CompilerParams(vmem_limit_bytes=VMEM_LIMIT),
    )(lam, q, q, k, v, bias_tiles, subln_g)


def _shift_rows(x, d):
    rolled = pltpu.roll(x, d, axis=0)
    t = lax.broadcasted_iota(I32, x.shape, 0)
    return jnp.where(t >= d, rolled, 0.0)


def _pool_kernel(u_ref, w_ref, b_ref, scale_ref, p_ref):
    seq = u_ref.shape[0]
    t = lax.broadcasted_iota(I32, (seq, POOL_GROUP), 0)
    for g, window in enumerate(POOL_WINDOWS):
        cols = slice(g * POOL_GROUP, (g + 1) * POOL_GROUP)
        u = u_ref[:, cols]
        win_sum = u
        span = 1
        while span < window:
            win_sum = win_sum + _shift_rows(win_sum, span)
            span *= 2
        cnt = jnp.minimum(t + 1, window).astype(F32)
        pooled = win_sum / cnt - u
        y = _dot(pooled.astype(BF16), w_ref[g]) + b_ref[g]
        p_ref[:, cols] = (y * scale_ref[:, cols]).astype(BF16)


def _pool(u, w_pool_bf, b_pool, pool_scale, batch, seq):
    n = u.shape[0]
    n_groups = len(POOL_WINDOWS)
    return pl.pallas_call(
        _pool_kernel,
        grid=(batch,),
        in_specs=[
            pl.BlockSpec((seq, POOL_WIDTH), lambda b: (b, 0)),
            pl.BlockSpec((n_groups, POOL_GROUP, POOL_GROUP), lambda b: (0, 0, 0)),
            pl.BlockSpec((n_groups, 1, POOL_GROUP), lambda b: (0, 0, 0)),
            pl.BlockSpec((1, POOL_WIDTH), lambda b: (0, 0)),
        ],
        out_specs=pl.BlockSpec((seq, POOL_WIDTH), lambda b: (b, 0)),
        out_shape=jax.ShapeDtypeStruct((n, POOL_WIDTH), BF16),
        compiler_params=pltpu.CompilerParams(vmem_limit_bytes=VMEM_LIMIT),
    )(u, w_pool_bf, b_pool, pool_scale)


def _outproj_kernel(tiles_per_batch, o_lo_ref, o_hi_ref, p_ref, x_ref, mod_ref, wo_ref, g1_ref, b1_ref,
                    wrt_ref, br_ref, x1_ref, h2_ref, pos_ref, gate_ref, cnt_ref):
    tm = TOK_TILE
    ne = N_EXPERTS
    in_first_half = (pl.program_id(0) % tiles_per_batch) < tiles_per_batch // 2
    o = jnp.where(in_first_half, o_lo_ref[...], o_hi_ref[...])
    mix = _dot(o, wo_ref[:ATTN_WIDTH, :]) + _dot(p_ref[...], wo_ref[ATTN_WIDTH:, :])
    x1 = _layer_norm(DEEPNORM_ALPHA * x_ref[...] + mod_ref[2:3, :] * mix) * g1_ref[...] + b1_ref[...]
    x1_ref[...] = x1
    h2 = _layer_norm(x1) * (1.0 + mod_ref[4:5, :]) + mod_ref[3:4, :]
    h2_ref[...] = h2.astype(BF16)

    logits = _dot3_nt(wrt_ref[...], h2) + br_ref[...]
    e_idx = lax.broadcasted_iota(I32, (ne, tm), 0).astype(F32)
    top_v, sels = [], []
    for _ in range(TOP_K):
        mx = jnp.max(logits, axis=0, keepdims=True)
        pick = jnp.min(jnp.where(logits == mx, e_idx, float(ne)), axis=0, keepdims=True)
        sel = e_idx == pick
        logits = jnp.where(sel, -jnp.inf, logits)
        top_v.append(mx)
        sels.append(sel)
    exps = [jnp.exp(v - top_v[0]) for v in top_v]
    denom = exps[0] + exps[1] + exps[2] + exps[3]
    gate_ref[...] = jnp.concatenate([e / denom for e in exps], axis=0)

    onehot = jnp.zeros((ne, tm), F32)
    for sel in sels:
        onehot = onehot + jnp.where(sel, 1.0, 0.0)
    earlier = (lax.broadcasted_iota(I32, (tm, tm), 0) < lax.broadcasted_iota(I32, (tm, tm), 1))
    rank = _dot(onehot.astype(BF16), jnp.where(earlier, 1.0, 0.0).astype(BF16))
    cnt = jnp.sum(onehot, axis=1, keepdims=True)
    padded = jnp.floor((cnt + (SEG_ROWS - 1)) * (1.0 / SEG_ROWS)) * SEG_ROWS
    lower = (lax.broadcasted_iota(I32, (ne, ne), 1) < lax.broadcasted_iota(I32, (ne, ne), 0))
    seg_off = _dot(jnp.where(lower, 1.0, 0.0).astype(BF16),
                   jnp.broadcast_to(padded, (ne, tm)).astype(BF16))
    slot = seg_off + rank
    pos = [jnp.sum(jnp.where(sel, slot, 0.0), axis=0, keepdims=True) for sel in sels]
    pos_ref[...] = jnp.concatenate(pos, axis=0).astype(I32)
    cnt_ref[...] = jnp.broadcast_to(cnt, (ne, LANES)).astype(I32)


def _outproj(o_lo, o_hi, p, x2, mod3, w_out_bf, ln1_g, ln1_b, w_router_t, b_router_col, seq):
    n = x2.shape[0]
    assert TOK_TILE == ATT_TILE
    tiles_per_batch = seq // TOK_TILE
    half = tiles_per_batch // 2
    n_tiles = n // TOK_TILE
    row = lambda r: (r, 0)
    const = lambda r: (0, 0)

    def lo_tile(r):
        return ((r // tiles_per_batch) * half + jnp.minimum(r % tiles_per_batch, half - 1), 0)

    def hi_tile(r):
        return ((r // tiles_per_batch) * half
                + jnp.minimum(tiles_per_batch - 1 - r % tiles_per_batch, half - 1), 0)

    return pl.pallas_call(
        functools.partial(_outproj_kernel, tiles_per_batch),
        grid=(n_tiles,),
        in_specs=[
            pl.BlockSpec((TOK_TILE, ATTN_WIDTH), lo_tile),
            pl.BlockSpec((TOK_TILE, ATTN_WIDTH), hi_tile),
            pl.BlockSpec((TOK_TILE, POOL_WIDTH), row),
            pl.BlockSpec((TOK_TILE, D_MODEL), row),
            pl.BlockSpec((None, 6, D_MODEL), lambda r: (r // tiles_per_batch, 0, 0)),
            pl.BlockSpec((D_MODEL, D_MODEL), const),
            pl.BlockSpec((1, D_MODEL), const),
            pl.BlockSpec((1, D_MODEL), const),
            pl.BlockSpec((N_EXPERTS, D_MODEL), const),
            pl.BlockSpec((N_EXPERTS, 1), const),
        ],
        out_specs=[
            pl.BlockSpec((TOK_TILE, D_MODEL), row),
            pl.BlockSpec((TOK_TILE, D_MODEL), row),
            pl.BlockSpec((TOP_K, TOK_TILE), lambda r: (0, r)),
            pl.BlockSpec((TOP_K, TOK_TILE), lambda r: (0, r)),
            pl.BlockSpec((None, N_EXPERTS, LANES), lambda r: (r, 0, 0)),
        ],
        out_shape=[
            jax.ShapeDtypeStruct((n, D_MODEL), F32),
            jax.ShapeDtypeStruct((n, D_MODEL), BF16),
            jax.ShapeDtypeStruct((TOP_K, n), I32),
            jax.ShapeDtypeStruct((TOP_K, n), F32),
            jax.ShapeDtypeStruct((n_tiles, N_EXPERTS, LANES), I32),
        ],
        compiler_params=pltpu.CompilerParams(vmem_limit_bytes=VMEM_LIMIT),
    )(o_lo, o_hi, p, x2, mod3, w_out_bf, ln1_g, ln1_b, w_router_t, b_router_col)


def _segment_copies(tile, nch_ref, make_copy, op):
    def per_expert(e, carry):
        idx = tile * N_EXPERTS + e

        def per_chunk(c, carry2):
            copy = make_copy(idx, c)
            copy.start() if op == "start" else copy.wait()
            return carry2

        return lax.fori_loop(0, nch_ref[idx], per_chunk, carry)

    lax.fori_loop(0, N_EXPERTS, per_expert, 0)


def _dispatch_kernel(base_ref, nch_ref, off_ref, ztail_ref, nz_ref, h2_ref, pos_ref, xs_ref,
                     stage_ref, zero_ref, sem):
    tile = pl.program_id(0)
    seg = SEG_ROWS * ROW_SLABS

    @pl.when(tile == 0)
    def _():
        zero_ref[...] = jnp.zeros(zero_ref.shape, F32)

        def zero_copy(e, c):
            dst = pl.multiple_of((ztail_ref[e] + c * SEG_ROWS) * ROW_SLABS, ROW_SLABS)
            return pltpu.make_async_copy(zero_ref, xs_ref.at[pl.ds(dst, seg)], sem)

        for lo, hi in ((0, N_EXPERTS), (N_EXPERTS, N_EXPERTS + 1)):
            for op in ("start", "wait"):
                def per_expert(e, carry, op=op):
                    def per_chunk(c, carry2):
                        copy = zero_copy(e, c)
                        copy.start() if op == "start" else copy.wait()
                        return carry2
                    return lax.fori_loop(0, nz_ref[e], per_chunk, carry)
                lax.fori_loop(lo, hi, per_expert, 0)

    pos = pos_ref[...]
    r = lax.broadcasted_iota(I32, (STAGE_ROWS, TOK_TILE), 0)
    perm = jnp.where(r == pos[0:1], 1.0,
                     jnp.where(r == pos[1:2], 1.0,
                               jnp.where(r == pos[2:3], 1.0,
                                         jnp.where(r == pos[3:4], 1.0, 0.0))))
    sorted_rows = _dot(perm.astype(BF16), h2_ref[...])
    for s in range(ROW_SLABS):
        stage_ref[pl.ds(s, STAGE_ROWS, stride=ROW_SLABS), :] = sorted_rows[:, s * LANES:(s + 1) * LANES]

    def row_copy(idx, c):
        src = pl.multiple_of((off_ref[idx] + c * SEG_ROWS) * ROW_SLABS, ROW_SLABS)
        dst = pl.multiple_of((base_ref[idx] + c * SEG_ROWS) * ROW_SLABS, ROW_SLABS)
        return pltpu.make_async_copy(stage_ref.at[pl.ds(src, seg)], xs_ref.at[pl.ds(dst, seg)], sem)

    _segment_copies(tile, nch_ref, row_copy, "start")
    _segment_copies(tile, nch_ref, row_copy, "wait")


def _dispatch(base, nch, off, ztail, nz, h2, pos, n_rows):
    n = h2.shape[0]
    return pl.pallas_call(
        _dispatch_kernel,
        grid_spec=pltpu.PrefetchScalarGridSpec(
            num_scalar_prefetch=5,
            grid=(n // TOK_TILE,),
            in_specs=[
                pl.BlockSpec((TOK_TILE, D_MODEL), lambda t, *_: (t, 0)),
                pl.BlockSpec((TOP_K, TOK_TILE), lambda t, *_: (0, t)),
            ],
            out_specs=pl.BlockSpec(memory_space=pl.ANY),
            scratch_shapes=[
                pltpu.VMEM((STAGE_ROWS * ROW_SLABS, LANES), F32),
                pltpu.VMEM((SEG_ROWS * ROW_SLABS, LANES), F32),
                pltpu.SemaphoreType.DMA(()),
            ],
        ),
        out_shape=jax.ShapeDtypeStruct((n_rows * ROW_SLABS, LANES), F32),
        compiler_params=pltpu.CompilerParams(vmem_limit_bytes=VMEM_LIMIT),
    )(base, nch, off, ztail, nz, h2, pos)


def _ffn_kernel(be_ref, bidx_ref, first_ref, valid_ref, xs_ref, wgu_ref, bgu_ref, wd_ref, bd_ref,
                ys_ref, wgu_bf, wd_bf):
    j = pl.program_id(0)

    @pl.when(valid_ref[j] == 0)
    def _():
        ys_ref[...] = jnp.zeros(ys_ref.shape, F32)

    @pl.when(valid_ref[j] == 1)
    def _():
        @pl.when(first_ref[j] == 1)
        def _():
            wgu_bf[...] = wgu_ref[...].astype(BF16)
            wd_bf[...] = wd_ref[...].astype(BF16)

        x = jnp.concatenate(
            [xs_ref[pl.ds(s, FFN_ROWS, stride=ROW_SLABS), :] for s in range(ROW_SLABS)], axis=-1)
        hgu = _dot(x.astype(BF16), wgu_bf[...]) + bgu_ref[...]
        gate = jnp.minimum(hgu[:, :D_FF], SWIGLU_LIMIT)
        up = jnp.clip(hgu[:, D_FF:], -SWIGLU_LIMIT, SWIGLU_LIMIT)
        act = (up + 1.0) * (gate * jax.nn.sigmoid(SWIGLU_ALPHA * gate))
        y = _dot(act.astype(BF16), wd_bf[...]) + bd_ref[...]
        for s in range(ROW_SLABS):
            ys_ref[pl.ds(s, FFN_ROWS, stride=ROW_SLABS), :] = y[:, s * LANES:(s + 1) * LANES]


def _ffn(block_e, block_idx, block_first, block_valid, xs, w_gate_up, b_gate_up, w_down, b_down):
    n_blocks = block_e.shape[0]
    blk = FFN_ROWS * ROW_SLABS
    expert = lambda j, be, bi, bf, bv: (be[j], 0, 0)
    rows = lambda j, be, bi, bf, bv: (bi[j], 0)
    return pl.pallas_call(
        _ffn_kernel,
        grid_spec=pltpu.PrefetchScalarGridSpec(
            num_scalar_prefetch=4,
            grid=(n_blocks,),
            in_specs=[
                pl.BlockSpec((blk, LANES), rows),
                pl.BlockSpec((None, D_MODEL, 2 * D_FF), expert),
                pl.BlockSpec((None, 1, 2 * D_FF), expert),
                pl.BlockSpec((None, D_FF, D_MODEL), expert),
                pl.BlockSpec((None, 1, D_MODEL), expert),
            ],
            out_specs=pl.BlockSpec((blk, LANES), lambda j, be, bi, bf, bv: (j, 0)),
            scratch_shapes=[
                pltpu.VMEM((D_MODEL, 2 * D_FF), BF16),
                pltpu.VMEM((D_FF, D_MODEL), BF16),
            ],
        ),
        out_shape=jax.ShapeDtypeStruct(xs.shape, F32),
        compiler_params=pltpu.CompilerParams(vmem_limit_bytes=VMEM_LIMIT),
    )(block_e, block_idx, block_first, block_valid, xs, w_gate_up, b_gate_up, w_down, b_down)


def _combine_kernel(base_ref, nch_ref, off_ref, ys_ref, pos_ref, gate_ref, x1_ref, mod_ref,
                    g2_ref, b2_ref, out_ref, stage_ref, sem):
    tile = pl.program_id(0)
    seg = SEG_ROWS * ROW_SLABS

    @pl.when(tile == 0)
    def _():
        stage_ref[...] = jnp.zeros(stage_ref.shape, F32)

    def row_copy(idx, c):
        src = pl.multiple_of((base_ref[idx] + c * SEG_ROWS) * ROW_SLABS, ROW_SLABS)
        dst = pl.multiple_of((off_ref[idx] + c * SEG_ROWS) * ROW_SLABS, ROW_SLABS)
        return pltpu.make_async_copy(ys_ref.at[pl.ds(src, seg)], stage_ref.at[pl.ds(dst, seg)], sem)

    _segment_copies(tile, nch_ref, row_copy, "start")
    _segment_copies(tile, nch_ref, row_copy, "wait")

    y_sorted = jnp.concatenate(
        [stage_ref[pl.ds(s, STAGE_ROWS, stride=ROW_SLABS), :] for s in range(ROW_SLABS)], axis=-1)
    pos = pos_ref[...]
    gate = gate_ref[...]
    r = lax.broadcasted_iota(I32, (TOK_TILE, STAGE_ROWS), 1)
    weights = jnp.where(r == pos[:, 0:1], gate[:, 0:1],
                        jnp.where(r == pos[:, 1:2], gate[:, 1:2],
                                  jnp.where(r == pos[:, 2:3], gate[:, 2:3],
                                            jnp.where(r == pos[:, 3:4], gate[:, 3:4], 0.0))))
    y = _dot(weights.astype(BF16), y_sorted.astype(BF16))
    z = DEEPNORM_ALPHA * x1_ref[...] + mod_ref[5:6, :] * y
    out_ref[...] = _layer_norm(z) * g2_ref[...] + b2_ref[...]


def _combine(base, nch, off, ys, pos_t, gate_t, x1, mod3, ln2_g, ln2_b, seq):
    n = x1.shape[0]
    tiles_per_batch = seq // TOK_TILE
    row = lambda t, *_: (t, 0)
    const = lambda t, *_: (0, 0)
    return pl.pallas_call(
        _combine_kernel,
        grid_spec=pltpu.PrefetchScalarGridSpec(
            num_scalar_prefetch=3,
            grid=(n // TOK_TILE,),
            in_specs=[
                pl.BlockSpec(memory_space=pl.ANY),
                pl.BlockSpec((TOK_TILE, TOP_K), row),
                pl.BlockSpec((TOK_TILE, TOP_K), row),
                pl.BlockSpec((TOK_TILE, D_MODEL), row),
                pl.BlockSpec((None, 6, D_MODEL), lambda t, *_: (t // tiles_per_batch, 0, 0)),
                pl.BlockSpec((1, D_MODEL), const),
                pl.BlockSpec((1, D_MODEL), const),
            ],
            out_specs=pl.BlockSpec((TOK_TILE, D_MODEL), row),
            scratch_shapes=[
                pltpu.VMEM((STAGE_ROWS * ROW_SLABS, LANES), F32),
                pltpu.SemaphoreType.DMA(()),
            ],
        ),
        out_shape=jax.ShapeDtypeStruct((n, D_MODEL), F32),
        compiler_params=pltpu.CompilerParams(vmem_limit_bytes=VMEM_LIMIT),
    )(base, nch, off, ys, pos_t, gate_t, x1, mod3, ln2_g, ln2_b)


def _ceil_div(a, b):
    return (a + b - 1) // b


def _routing_plan(cnt, n_blocks):
    counts = jnp.sum(cnt, axis=0)
    region = _ceil_div(counts + (SEG_ROWS - 1), FFN_ROWS) * FFN_ROWS
    rend = jnp.cumsum(region)
    rstart = rend - region
    base = rstart[None, :] + (jnp.cumsum(cnt, axis=0) - cnt)
    nch = _ceil_div(cnt, SEG_ROWS)
    padded = nch * SEG_ROWS
    off = jnp.cumsum(padded, axis=1) - padded
    n_rows = n_blocks * FFN_ROWS
    ztail = jnp.concatenate([rstart + counts, rend[-1:]])
    nz = jnp.concatenate([_ceil_div(rend - ztail[:-1], SEG_ROWS), (n_rows - rend[-1:]) // SEG_ROWS])
    blk_start = jnp.arange(n_blocks, dtype=I32) * FFN_ROWS
    block_e = jnp.minimum(jnp.sum(blk_start[:, None] >= rend[None, :], axis=1), N_EXPERTS - 1).astype(I32)
    block_valid = (blk_start < rend[-1]).astype(I32)
    last_valid = rend[-1] // FFN_ROWS - 1
    block_idx = jnp.minimum(jnp.arange(n_blocks, dtype=I32), last_valid).astype(I32)
    block_first = (blk_start == rstart[block_e]).astype(I32)
    flat = lambda a: a.reshape(-1).astype(I32)
    return (flat(base), flat(nch), flat(off), flat(ztail), flat(nz),
            block_e, block_idx, block_first, block_valid)


def kernel(x, c, w_ada, b_ada, w_in, lambda_q1, lambda_k1, lambda_q2, lambda_k2, subln_g, rel_bias,
           w_pool, b_pool, pool_scale, w_out, ln1_g, ln1_b, w_router, b_router, w_gate_up, b_gate_up,
           w_down, b_down, ln2_g, ln2_b):
    batch, seq, d = x.shape
    n = batch * seq
    assert d == D_MODEL and w_ada.shape[0] == DEPTH == 1
    assert seq % ROW_TILE == 0 and seq % ATT_TILE == 0 and seq % TOK_TILE == 0
    x2 = x.reshape(n, d)

    mod3 = _ada(c, w_ada[0], b_ada).reshape(batch, 6, d)

    q, k, v, u = _inproj(x2, mod3, w_in[0].astype(BF16), seq)

    lam_init = 0.8 - 0.6 * math.exp(-0.3 * 0)
    lam = (jnp.exp(jnp.sum(lambda_q1[0].astype(F32) * lambda_k1[0].astype(F32)))
           - jnp.exp(jnp.sum(lambda_q2[0].astype(F32) * lambda_k2[0].astype(F32))) + lam_init)
    bucket_tiles, tile_ids = _bucket_tiles(seq // ATT_TILE)
    bias_tiles = _bias_tiles(rel_bias, bucket_tiles)
    o_lo, o_hi = _attention(lam.reshape(1), q, k, v, bias_tiles, tile_ids, subln_g, batch, seq)

    p = _pool(u, w_pool[0].astype(BF16), b_pool[0][:, None, :], pool_scale, batch, seq)

    x1, h2, pos, gates, cnt_l = _outproj(
        o_lo, o_hi, p, x2, mod3, w_out[0].astype(BF16), ln1_g, ln1_b,
        w_router[0].T, b_router[0][:, None], seq)

    n_assign = n * TOP_K
    n_blocks = _ceil_div(n_assign + N_EXPERTS * (SEG_ROWS - 1 + FFN_ROWS - 1) + SEG_ROWS, FFN_ROWS)
    (base, nch, off, ztail, nz, block_e, block_idx, block_first, block_valid) = _routing_plan(
        cnt_l[:, :, 0], n_blocks)

    xs = _dispatch(base, nch, off, ztail, nz, h2, pos, n_blocks * FFN_ROWS)
    ys = _ffn(block_e, block_idx, block_first, block_valid, xs,
              w_gate_up[0], b_gate_up[0][:, None, :], w_down[0], b_down[0][:, None, :])
    out = _combine(base, nch, off, ys, pos.T, gates.T, x1, mod3, ln2_g, ln2_b, seq)
    return out.reshape(batch, seq, d)
```
